```python
import math
import jax, jax.numpy as jnp
from jax import lax
import numpy as np

D_MODEL = 4096
BATCH = 2
SEQ = 8192
DEPTH = 2

N_A_LAYERS = (DEPTH + 1) // 2
N_B_LAYERS = DEPTH - N_A_LAYERS

FOX_HEAD_DIM = 128
FOX_HEADS = D_MODEL // FOX_HEAD_DIM
FORGET_BIAS_LO = 2.0
FORGET_BIAS_HI = 7.0

MLA_HEADS = D_MODEL // 128
MLA_Q_LORA = 1024
MLA_KV_LORA = 512
MLA_NOPE = 128
MLA_ROPE = 64
MLA_V = 128
ROPE_THETA = 10000.0

FFN_DIM = 11008
CONV_WIDTH = 3

Q_BLOCK = 128
RMS_EPS = 1e-6

kernel_name = "yoco_fox_mla_convffn_sandwich"


def _rms_norm(x, gain):
    xf = x.astype(jnp.float32)
    xf = xf * lax.rsqrt(jnp.mean(xf * xf, axis=-1, keepdims=True) + RMS_EPS)
    return (xf * gain.astype(jnp.float32)).astype(x.dtype)


def _rope_angles(positions):
    inv_freq = ROPE_THETA ** (-jnp.arange(0, MLA_ROPE, 2, dtype=jnp.float32) / MLA_ROPE)
    ang = positions.astype(jnp.float32)[..., None] * inv_freq
    return jnp.cos(ang), jnp.sin(ang)


def _rope(x, cos, sin):
    xf = x.astype(jnp.float32)
    x1, x2 = jnp.split(xf, 2, axis=-1)
    return jnp.concatenate([x1 * cos - x2 * sin, x1 * sin + x2 * cos], axis=-1).astype(x.dtype)


def _causal_block_attention(q, k, v, scale, q_rope=None, k_rope=None, cum_log_f=None):
    B, S, H, _ = q.shape
    Dv = v.shape[-1]
    nb = S // Q_BLOCK

    def to_blocks(a):
        return a.reshape((B, nb, Q_BLOCK) + a.shape[2:]).swapaxes(0, 1)

    key_pos = jnp.arange(S)
    c_t = None if cum_log_f is None else cum_log_f.astype(jnp.float32).transpose(0, 2, 1)

    def one_block(args):
        i, q_i, qr_i = args
        s = jnp.einsum('bqhd,bkhd->bhqk', q_i, k, preferred_element_type=jnp.float32)
        if k_rope is not None:
            s = s + jnp.einsum('bqhr,bkr->bhqk', qr_i, k_rope, preferred_element_type=jnp.float32)
        s = s * scale
        if c_t is not None:
            c_q = lax.dynamic_slice_in_dim(c_t, i * Q_BLOCK, Q_BLOCK, axis=2)
            s = s + (c_q[..., None] - c_t[:, :, None, :])
        query_pos = i * Q_BLOCK + jnp.arange(Q_BLOCK)
        s = jnp.where(key_pos[None, :] <= query_pos[:, None], s, -jnp.inf)
        p = jax.nn.softmax(s, axis=-1)
        return jnp.einsum('bhqk,bkhd->bqhd', p.astype(v.dtype), v)

    qr_blocks = None if q_rope is None else to_blocks(q_rope)
    out = lax.map(one_block, (jnp.arange(nb), to_blocks(q), qr_blocks))
    return out.swapaxes(0, 1).reshape(B, S, H * Dv)


def _fox_attention(xn, w_qkvf, b_f, w_o):
    B, S, D = xn.shape
    qkvf = xn @ w_qkvf
    q = qkvf[..., :D].reshape(B, S, FOX_HEADS, FOX_HEAD_DIM)
    k = qkvf[..., D:2 * D].reshape(B, S, FOX_HEADS, FOX_HEAD_DIM)
    v = qkvf[..., 2 * D:3 * D].reshape(B, S, FOX_HEADS, FOX_HEAD_DIM)
    log_f = jax.nn.log_sigmoid(qkvf[..., 3 * D:].astype(jnp.float32) + b_f.astype(jnp.float32))
    cum_log_f = jnp.cumsum(log_f, axis=1)
    o = _causal_block_attention(q, k, v, 1.0 / math.sqrt(FOX_HEAD_DIM), cum_log_f=cum_log_f)
    return o @ w_o


def _mla_shared_kv(h, cos, sin, kv_in_norm, w_dkv, kv_norm, w_uk, w_uv):
    B, S, _ = h.shape
    s = _rms_norm(h, kv_in_norm)
    ckv = s @ w_dkv
    c_kv = _rms_norm(ckv[..., :MLA_KV_LORA], kv_norm)
    k_pe = _rope(ckv[..., MLA_KV_LORA:], cos, sin)
    k_nope = (c_kv @ w_uk).reshape(B, S, MLA_HEADS, MLA_NOPE)
    v = (c_kv @ w_uv).reshape(B, S, MLA_HEADS, MLA_V)
    return k_nope, k_pe, v


def _mla_attention(xn, cos, sin, k_nope, k_pe, v, w_dq, q_norm, w_uq, w_o):
    B, S, _ = xn.shape
    c_q = _rms_norm(xn @ w_dq, q_norm)
    q = (c_q @ w_uq).reshape(B, S, MLA_HEADS, MLA_NOPE + MLA_ROPE)
    q_nope = q[..., :MLA_NOPE]
    q_pe = _rope(q[..., MLA_NOPE:], cos[:, :, None, :], sin[:, :, None, :])
    scale = 1.0 / math.sqrt(MLA_NOPE + MLA_ROPE)
    o = _causal_block_attention(q_nope, k_nope, v, scale, q_rope=q_pe, k_rope=k_pe)
    return o @ w_o


def _conv_gated_mlp(xn, w_gate, conv_w, conv_b, w_up, w_down):
    g = xn @ w_gate
    g = lax.conv_general_dilated(
        g, conv_w.reshape(CONV_WIDTH, 1, FFN_DIM).astype(g.dtype),
        window_strides=(1,), padding=[(CONV_WIDTH - 1, 0)],
        dimension_numbers=('NWC', 'WIO', 'NWC'), feature_group_count=FFN_DIM) + conv_b
    return (jax.nn.silu(g) * (xn @ w_up)) @ w_down


def setup_inputs(seed: int = 0) -> dict:
    key = jax.random.key(seed)
    ks = jax.random.split(key, 24)

    def nrm(k, shape, fan_in):
        return jax.random.normal(k, shape, jnp.float32) * (fan_in ** -0.5)

    def gain(k, shape):
        return 1.0 + 0.02 * jax.random.normal(k, shape, jnp.float32)

    D, F = D_MODEL, FFN_DIM
    x = jax.random.normal(ks[0], (BATCH, SEQ, D), jnp.float32)
    offset = jax.random.randint(ks[1], (BATCH, 1), 0, 4096, dtype=jnp.int32)
    positions = offset + jnp.arange(SEQ, dtype=jnp.int32)[None, :]
    fox_b_f = (jnp.linspace(FORGET_BIAS_LO, FORGET_BIAS_HI, FOX_HEADS, dtype=jnp.float32)[None, :]
               + 0.1 * jax.random.normal(ks[8], (N_A_LAYERS, FOX_HEADS), jnp.float32))
    return {
        "x": x,
        "positions": positions,
        "norm_pre_mix": gain(ks[2], (DEPTH, D)),
        "norm_post_mix": gain(ks[3], (DEPTH, D)),
        "norm_pre_ffn": gain(ks[4], (DEPTH, D)),
        "norm_post_ffn": gain(ks[5], (DEPTH, D)),
        "fox_w_qkvf": nrm(ks[6], (N_A_LAYERS, D, 3 * D + FOX_HEADS), D),
        "fox_b_f": fox_b_f,
        "fox_w_o": nrm(ks[7], (N_A_LAYERS, D, D), D),
        "mla_w_dq": nrm(ks[9], (N_B_LAYERS, D, MLA_Q_LORA), D),
        "mla_q_norm": gain(ks[10], (N_B_LAYERS, MLA_Q_LORA)),
        "mla_w_uq": nrm(ks[11], (N_B_LAYERS, MLA_Q_LORA, MLA_HEADS * (MLA_NOPE + MLA_ROPE)), MLA_Q_LORA),
        "mla_w_o": nrm(ks[12], (N_B_LAYERS, MLA_HEADS * MLA_V, D), MLA_HEADS * MLA_V),
        "kv_in_norm": gain(ks[13], (D,)),
        "mla_w_dkv": nrm(ks[14], (D, MLA_KV_LORA + MLA_ROPE), D),
        "mla_kv_norm": gain(ks[15], (MLA_KV_LORA,)),
        "mla_w_uk": nrm(ks[16], (MLA_KV_LORA, MLA_HEADS * MLA_NOPE), MLA_KV_LORA),
        "mla_w_uv": nrm(ks[17], (MLA_KV_LORA, MLA_HEADS * MLA_V), MLA_KV_LORA),
        "ffn_w_gate": nrm(ks[18], (DEPTH, D, F), D),
        "ffn_conv_w": nrm(ks[19], (DEPTH, CONV_WIDTH, F), CONV_WIDTH),
        "ffn_conv_b": 0.02 * jax.random.normal(ks[20], (DEPTH, F), jnp.float32),
        "ffn_w_up": nrm(ks[21], (DEPTH, D, F), D),
        "ffn_w_down": nrm(ks[22], (DEPTH, F, D), F),
    }


def reference(x, positions, norm_pre_mix, norm_post_mix, norm_pre_ffn, norm_post_ffn,
              fox_w_qkvf, fox_b_f, fox_w_o,
              mla_w_dq, mla_q_norm, mla_w_uq, mla_w_o,
              kv_in_norm, mla_w_dkv, mla_kv_norm, mla_w_uk, mla_w_uv,
              ffn_w_gate, ffn_conv_w, ffn_conv_b, ffn_w_up, ffn_w_down):
    cos, sin = _rope_angles(positions)
    h = x
    shared_kv = None
    for layer in range(DEPTH):
        xn = _rms_norm(h, norm_pre_mix[layer])
        if layer < N_A_LAYERS:
            mix = _fox_attention(xn, fox_w_qkvf[layer], fox_b_f[layer], fox_w_o[layer])
        else:
            if shared_kv is None:
                shared_kv = _mla_shared_kv(h, cos, sin, kv_in_norm, mla_w_dkv, mla_kv_norm,
                                           mla_w_uk, mla_w_uv)
            k_nope, k_pe, v = shared_kv
            j = layer - N_A_LAYERS
            mix = _mla_attention(xn, cos, sin, k_nope, k_pe, v,
                                 mla_w_dq[j], mla_q_norm[j], mla_w_uq[j], mla_w_o[j])
        h = h + _rms_norm(mix, norm_post_mix[layer])
        f = _conv_gated_mlp(_rms_norm(h, norm_pre_ffn[layer]), ffn_w_gate[layer], ffn_conv_w[layer],
                            ffn_conv_b[layer], ffn_w_up[layer], ffn_w_down[layer])
        h = h + _rms_norm(f, norm_post_ffn[layer])
    return h
```

```python
import functools
import math

import jax
import jax.numpy as jnp
from jax import lax
from jax.experimental import pallas as pl
from jax.experimental.pallas import tpu as pltpu

RMS_EPS = 1e-6
ROPE_THETA = 10000.0
FORGET_SPLITS = 3
MASK_VALUE = -1e30

LANES = 128
SUBLANES = 8
V7X_VMEM_BYTES = 64 * 1024 * 1024
VMEM_RESERVE_BYTES = 6 * 1024 * 1024
MXU_DEPTH = 256

BF16 = jnp.bfloat16
F32 = jnp.float32


def _vmem_limit(block_bytes, scratch_bytes, temp_bytes):
    need = 2 * block_bytes + scratch_bytes + temp_bytes
    return int(min(max(need, 16 * 1024 * 1024), V7X_VMEM_BYTES - VMEM_RESERVE_BYTES))


def _pick(n, prefs):
    for p in prefs:
        if n % p == 0:
            return p
    return n


def _rms(y, gain):
    ms = jnp.mean(y * y, axis=-1, keepdims=True)
    return y * lax.rsqrt(ms + RMS_EPS) * gain


def _mm_kernel(*refs, nk, pre_norm, post_norm, has_res, mul_period, acc_in_out):
    it = iter(refs)
    x_ref = next(it)
    w_ref = next(it)
    pg_ref = next(it) if pre_norm else None
    og_ref = next(it) if post_norm else None
    res_ref = next(it) if has_res else None
    mul_ref = next(it) if mul_period else None
    o_ref = next(it)
    xn_ref = next(it) if pre_norm else None
    acc_ref = o_ref if acc_in_out else (next(it) if nk > 1 else None)

    j = pl.program_id(1)
    k = pl.program_id(2)

    if pre_norm:
        @pl.when(j == 0)
        def _():
            xn_ref[...] = _rms(x_ref[...], pg_ref[...]).astype(BF16)
        lhs = xn_ref[...]
    else:
        lhs = x_ref[...]

    part = jnp.dot(lhs, w_ref[...], preferred_element_type=F32)

    def epilogue(y):
        if post_norm:
            y = _rms(y, og_ref[...])
        if has_res:
            y = res_ref[...] + y
        if mul_period:
            for p in range(y.shape[1] // mul_period):
                sl = slice(p * mul_period, (p + 1) * mul_period)
                o_ref[:, sl] = (y[:, sl] * mul_ref[...]).astype(o_ref.dtype)
        else:
            o_ref[...] = y.astype(o_ref.dtype)

    if nk == 1:
        epilogue(part)
    else:
        @pl.when(k == 0)
        def _():
            acc_ref[...] = part

        @pl.when(k > 0)
        def _():
            acc_ref[...] += part

        @pl.when(k == nk - 1)
        def _():
            epilogue(acc_ref[...])


def _mm(x, w, *, out_dtype, pre_gain=None, post_gain=None, res=None, mul=None,
        bm=512, bn=None, bk=None, name="mm"):
    M, K = x.shape
    N = w.shape[1]
    bm = _pick(M, (bm, 256, 128, 64, 32, 16, 8))
    if post_gain is not None:
        bn = N
    elif bn is None:
        bn = _pick(N, (1024, 512, 256, 128))
    if pre_gain is not None or bk is None:
        bk = K
    else:
        bk = _pick(K, (bk, 256, 128))
    assert M % bm == 0 and N % bn == 0 and K % bk == 0
    nk = K // bk
    pre_norm, post_norm, has_res = pre_gain is not None, post_gain is not None, res is not None
    mul_period = 0 if mul is None else mul.shape[1]
    acc_in_out = nk > 1 and jnp.dtype(out_dtype) == jnp.dtype(F32)

    ins = [x, w]
    specs = [pl.BlockSpec((bm, bk), lambda i, j, k: (i, k)),
             pl.BlockSpec((bk, bn), lambda i, j, k: (k, j))]
    blk = bm * bk * x.dtype.itemsize + bk * bn * w.dtype.itemsize
    if pre_norm:
        ins.append(pre_gain.reshape(1, K).astype(F32))
        specs.append(pl.BlockSpec((1, K), lambda i, j, k: (0, 0)))
    if post_norm:
        ins.append(post_gain.reshape(1, N).astype(F32))
        specs.append(pl.BlockSpec((1, bn), lambda i, j, k: (0, j)))
    if has_res:
        ins.append(res)
        specs.append(pl.BlockSpec((bm, bn), lambda i, j, k: (i, j)))
        blk += bm * bn * res.dtype.itemsize
    if mul_period:
        assert bn % mul_period == 0
        ins.append(mul)
        specs.append(pl.BlockSpec((bm, mul_period), lambda i, j, k: (i, 0)))
        blk += bm * mul_period * 4
    blk += bm * bn * jnp.dtype(out_dtype).itemsize

    scratch, scratch_bytes = [], 0
    if pre_norm:
        scratch.append(pltpu.VMEM((bm, K), BF16))
        scratch_bytes += bm * K * 2
    if nk > 1 and not acc_in_out:
        scratch.append(pltpu.VMEM((bm, bn), F32))
        scratch_bytes += bm * bn * 4
    temp = 3 * bm * bn * 4 + (bm * K * 4 if pre_norm else 0)

    return pl.pallas_call(
        functools.partial(_mm_kernel, nk=nk, pre_norm=pre_norm, post_norm=post_norm,
                          has_res=has_res, mul_period=mul_period, acc_in_out=acc_in_out),
        grid=(M // bm, N // bn, nk),
        in_specs=specs,
        out_specs=pl.BlockSpec((bm, bn), lambda i, j, k: (i, j)),
        out_shape=jax.ShapeDtypeStruct((M, N), out_dtype),
        scratch_shapes=scratch,
        compiler_params=pltpu.CompilerParams(
            dimension_semantics=("arbitrary", "arbitrary", "arbitrary"),
            vmem_limit_bytes=_vmem_limit(blk, scratch_bytes, temp)),
        name=name,
    )(*ins)


def _ffn_up_kernel(h_ref, gain_ref, wg_ref, wu_ref, cw_ref, cb_ref, a_ref, xn_ref, tail_ref,
                   *, blocks_per_seq):
    i = pl.program_id(0)
    j = pl.program_id(1)

    @pl.when(j == 0)
    def _():
        xn_ref[...] = _rms(h_ref[...], gain_ref[...]).astype(BF16)

    @pl.when(i % blocks_per_seq == 0)
    def _():
        tail_ref[j] = jnp.zeros(tail_ref.shape[1:], F32)

    xn = xn_ref[...]
    g = jnp.dot(xn, wg_ref[...], preferred_element_type=F32)
    u = jnp.dot(xn, wu_ref[...], preferred_element_type=F32)
    bm = g.shape[0]
    prev = tail_ref[j]
    tail_ref[j] = g[bm - SUBLANES:, :]
    row = lax.broadcasted_iota(jnp.int32, g.shape, 0)
    g1 = jnp.where(row == 0, prev[SUBLANES - 1:SUBLANES, :], pltpu.roll(g, 1, axis=0))
    g2 = jnp.where(row == 0, prev[SUBLANES - 2:SUBLANES - 1, :],
                   jnp.where(row == 1, prev[SUBLANES - 1:SUBLANES, :], pltpu.roll(g, 2, axis=0)))
    cw = cw_ref[...]
    gc = g2 * cw[0:1, :] + g1 * cw[1:2, :] + g * cw[2:3, :] + cb_ref[...]
    a_ref[...] = (gc * jax.nn.sigmoid(gc) * u).astype(a_ref.dtype)


def _ffn_up(h, gain, wg, wu, conv_w, conv_b, seq_len, *, bm=512, fc=512):
    T, D = h.shape
    Fp = wg.shape[1]
    bm = _pick(seq_len, (bm, 256, 128, 64, 32, 16, 8))
    fc = _pick(Fp, (fc, 256, 128))
    nf = Fp // fc
    blk = bm * D * 4 + 2 * D * fc * 2 + 4 * fc * 4 + bm * fc * 2
    scratch_bytes = bm * D * 2 + nf * SUBLANES * fc * 4
    temp = bm * D * 4 + 8 * bm * fc * 4
    return pl.pallas_call(
        functools.partial(_ffn_up_kernel, blocks_per_seq=seq_len // bm),
        grid=(T // bm, nf),
        in_specs=[pl.BlockSpec((bm, D), lambda i, j: (i, 0)),
                  pl.BlockSpec((1, D), lambda i, j: (0, 0)),
                  pl.BlockSpec((D, fc), lambda i, j: (0, j)),
                  pl.BlockSpec((D, fc), lambda i, j: (0, j)),
                  pl.BlockSpec((3, fc), lambda i, j: (0, j)),
                  pl.BlockSpec((1, fc), lambda i, j: (0, j))],
        out_specs=pl.BlockSpec((bm, fc), lambda i, j: (i, j)),
        out_shape=jax.ShapeDtypeStruct((T, Fp), BF16),
        scratch_shapes=[pltpu.VMEM((bm, D), BF16), pltpu.VMEM((nf, SUBLANES, fc), F32)],
        compiler_params=pltpu.CompilerParams(
            dimension_semantics=("arbitrary", "arbitrary"),
            vmem_limit_bytes=_vmem_limit(blk, scratch_bytes, temp)),
        name="ffn_up",
    )(h, gain.reshape(1, D).astype(F32), wg, wu, conv_w, conv_b.reshape(1, Fp))


def _cumgate_kernel(f_ref, b_ref, c_ref, carry_ref):
    @pl.when(pl.program_id(1) == 0)
    def _():
        carry_ref[...] = jnp.zeros(carry_ref.shape, F32)

    x = f_ref[...] + b_ref[...]
    lf = jnp.minimum(x, 0.0) - jnp.log1p(jnp.exp(-jnp.abs(x)))
    rt = lf.shape[0]
    tri = (lax.broadcasted_iota(jnp.int32, (rt, rt), 0)
           >= lax.broadcasted_iota(jnp.int32, (rt, rt), 1)).astype(BF16)
    y = carry_ref[0:1, :]
    rem = lf
    for _ in range(FORGET_SPLITS):
        piece = rem.astype(BF16)
        y = y + jnp.dot(tri, piece, preferred_element_type=F32)
        rem = rem - piece.astype(F32)
    c_ref[...] = y
    carry_ref[0:1, :] = y[rt - 1:rt, :]


def _cumgate(flog, bias, batch, seq_len):
    T, W = flog.shape
    rt = _pick(seq_len, (256, 128, 64, 32, 16, 8))
    nt = seq_len // rt
    return pl.pallas_call(
        _cumgate_kernel,
        grid=(batch, nt),
        in_specs=[pl.BlockSpec((rt, W), lambda b, t: (b * nt + t, 0)),
                  pl.BlockSpec((1, W), lambda b, t: (0, 0))],
        out_specs=pl.BlockSpec((rt, W), lambda b, t: (b * nt + t, 0)),
        out_shape=jax.ShapeDtypeStruct((T, W), F32),
        scratch_shapes=[pltpu.VMEM((SUBLANES, W), F32)],
        compiler_params=pltpu.CompilerParams(dimension_semantics=("arbitrary", "arbitrary")),
        name="cumgate",
    )(flog, bias)


def _rope_tables_kernel(ang_ref, sgn_ref, selc_ref, sels_ref, ta_ref, tb_ref, tq_ref):
    c = jnp.cos(ang_ref[...])
    s = jnp.sin(ang_ref[...])
    ta_ref[...] = c
    tb_ref[...] = s * sgn_ref[...]
    tq_ref[...] = c * selc_ref[...] + s * sels_ref[...]


def _rope_tables(ang4, sgn, selc, sels):
    T, W = ang4.shape
    bt = _pick(T, (1024, 512, 256, 128, 64, 32, 16, 8))
    row = pl.BlockSpec((bt, W), lambda i: (i, 0))
    const = pl.BlockSpec((1, W), lambda i: (0, 0))
    return pl.pallas_call(
        _rope_tables_kernel,
        grid=(T // bt,),
        in_specs=[row, const, const, const],
        out_specs=[row, row, row],
        out_shape=[jax.ShapeDtypeStruct((T, W), F32)] * 3,
        compiler_params=pltpu.CompilerParams(dimension_semantics=("arbitrary",)),
        name="rope_tables",
    )(ang4, sgn, selc, sels)


def _dkv_kernel(h_ref, gin_ref, w_ref, gkv_ref, ta_ref, tb_ref, ckv_ref, kr_ref, *, lat):
    xn = _rms(h_ref[...], gin_ref[...]).astype(BF16)
    y = jnp.dot(xn, w_ref[...], preferred_element_type=F32)
    ckv_ref[...] = _rms(y[:, :lat], gkv_ref[...]).astype(ckv_ref.dtype)
    kr = y[:, lat:lat + LANES] * ta_ref[...] + y[:, lat + LANES:lat + 2 * LANES] * tb_ref[...]
    kr_ref[...] = kr.astype(kr_ref.dtype)


def _dkv(h, gin, w, gkv, ta, tb, lat, *, bm=512):
    T, D = h.shape
    Nw = w.shape[1]
    bm = _pick(T, (bm, 256, 128, 64, 32, 16, 8))
    blk = bm * D * 4 + D * Nw * 2 + 2 * bm * LANES * 4 + bm * (lat + LANES) * 2
    temp = bm * D * 6 + 2 * bm * Nw * 4
    return pl.pallas_call(
        functools.partial(_dkv_kernel, lat=lat),
        grid=(T // bm,),
        in_specs=[pl.BlockSpec((bm, D), lambda i: (i, 0)),
                  pl.BlockSpec((1, D), lambda i: (0, 0)),
                  pl.BlockSpec((D, Nw), lambda i: (0, 0)),
                  pl.BlockSpec((1, lat), lambda i: (0, 0)),
                  pl.BlockSpec((bm, LANES), lambda i: (i, 0)),
                  pl.BlockSpec((bm, LANES), lambda i: (i, 0))],
        out_specs=[pl.BlockSpec((bm, lat), lambda i: (i, 0)),
                   pl.BlockSpec((bm, LANES), lambda i: (i, 0))],
        out_shape=[jax.ShapeDtypeStruct((T, lat), BF16), jax.ShapeDtypeStruct((T, LANES), BF16)],
        compiler_params=pltpu.CompilerParams(
            dimension_semantics=("arbitrary",),
            vmem_limit_bytes=_vmem_limit(blk, 0, temp)),
        name="mla_dkv",
    )(h, gin.reshape(1, D).astype(F32), w, gkv.reshape(1, lat).astype(F32), ta, tb)


def _attn_kernel(q_ref, ka_ref, kb_ref, v_ref, o_ref, kcat, vt, qt, m_s, l_s, acc_s,
                 *, ones_rows):
    qi = pl.program_id(2)
    nch, ch, _ = kcat.shape
    bq = q_ref.shape[0]
    qw = q_ref.shape[1]

    @pl.when(qi == 0)
    def _():
        def build(c, carry):
            r = pl.multiple_of(c * ch, ch)
            kcat[c, :, 0:LANES] = ka_ref[pl.ds(r, ch), :]
            kcat[c, :, LANES:2 * LANES] = kb_ref[pl.ds(r, ch), :]
            vt[c] = v_ref[pl.ds(r, ch), :].astype(F32).T.astype(BF16)
            return carry
        lax.fori_loop(0, nch, build, 0)

    qt[0:qw, :] = q_ref[...].astype(F32).T.astype(BF16)
    if ones_rows:
        rowi = lax.broadcasted_iota(jnp.int32, (LANES, bq), 0)
        qt[LANES:2 * LANES, :] = jnp.where(rowi < ones_rows, 1.0, 0.0).astype(BF16)
    m_s[...] = jnp.full(m_s.shape, MASK_VALUE, F32)
    l_s[...] = jnp.zeros(l_s.shape, F32)
    acc_s[...] = jnp.zeros(acc_s.shape, F32)

    def chunk(c, masked):
        s = jnp.dot(kcat[c], qt[...], preferred_element_type=F32)
        if masked:
            key = lax.broadcasted_iota(jnp.int32, s.shape, 0)
            qry = lax.broadcasted_iota(jnp.int32, s.shape, 1)
            s = jnp.where(key <= qry, s, MASK_VALUE)
        m_prev = m_s[...]
        m_new = jnp.maximum(m_prev, jnp.max(s, axis=0, keepdims=True))
        alpha = jnp.exp(m_prev - m_new)
        p = jnp.exp(s - m_new)
        l_s[...] = alpha * l_s[...] + jnp.sum(p, axis=0, keepdims=True)
        acc_s[...] = alpha * acc_s[...] + jnp.dot(vt[c], p.astype(BF16),
                                                  preferred_element_type=F32)
        m_s[...] = m_new

    def off_diag(c, carry):
        chunk(c, False)
        return carry
    lax.fori_loop(0, qi, off_diag, 0)
    chunk(qi, True)

    o_ref[...] = (acc_s[...] / l_s[...]).T.astype(o_ref.dtype)


def _attention(q_arr, q_cols, ka_arr, ka_col0, kb_arr, kb_per_head, v_arr, v_col0,
               batch, seq_len, heads, *, ones_rows, bq=512):
    T = batch * seq_len
    bq = _pick(seq_len, (bq, 256, 128))
    nq = seq_len // bq
    blk = bq * q_cols * 2 + 3 * seq_len * LANES * 2 + bq * LANES * 2
    scratch_bytes = seq_len * 2 * LANES * 2 + seq_len * LANES * 2 + 2 * LANES * bq * 2 \
        + 2 * SUBLANES * bq * 4 + LANES * bq * 4
    temp = 6 * bq * bq * 4
    kb_map = (lambda b, h, i: (b * heads + h, 0)) if kb_per_head else (lambda b, h, i: (b, 0))
    return pl.pallas_call(
        functools.partial(_attn_kernel, ones_rows=ones_rows),
        grid=(batch, heads, nq),
        in_specs=[pl.BlockSpec((bq, q_cols), lambda b, h, i: (b * nq + i, h)),
                  pl.BlockSpec((seq_len, LANES), lambda b, h, i: (b, ka_col0 + h)),
                  pl.BlockSpec((seq_len, LANES), kb_map),
                  pl.BlockSpec((seq_len, LANES), lambda b, h, i: (b, v_col0 + h))],
        out_specs=pl.BlockSpec((bq, LANES), lambda b, h, i: (b * nq + i, h)),
        out_shape=jax.ShapeDtypeStruct((T, heads * LANES), BF16),
        scratch_shapes=[pltpu.VMEM((nq, bq, 2 * LANES), BF16),
                        pltpu.VMEM((nq, LANES, bq), BF16),
                        pltpu.VMEM((2 * LANES, bq), BF16),
                        pltpu.VMEM((1, bq), F32),
                        pltpu.VMEM((1, bq), F32),
                        pltpu.VMEM((LANES, bq), F32)],
        compiler_params=pltpu.CompilerParams(
            dimension_semantics=("arbitrary", "arbitrary", "arbitrary"),
            vmem_limit_bytes=_vmem_limit(blk, scratch_bytes, temp)),
        name="attention",
    )(q_arr, ka_arr, kb_arr, v_arr)


def _split_bf16(x):
    pieces, rem = [], x
    for _ in range(FORGET_SPLITS - 1):
        t = rem * (2.0 ** 16 + 1.0)
        hi = t - (t - rem)
        pieces.append(hi.astype(BF16))
        rem = rem - hi
    pieces.append(rem.astype(BF16))
    return pieces


def _pad_cols(w, mult):
    pad = (-w.shape[-1]) % mult
    return w if pad == 0 else jnp.pad(w, [(0, 0)] * (w.ndim - 1) + [(0, pad)])


def _ffn(h, layer, seq_len, norm_pre_ffn, norm_post_ffn, ffn_w_gate, ffn_conv_w, ffn_conv_b,
         ffn_w_up, ffn_w_down):
    fpad = 1024
    wg = _pad_cols(ffn_w_gate[layer], fpad).astype(BF16)
    wu = _pad_cols(ffn_w_up[layer], fpad).astype(BF16)
    cw = _pad_cols(ffn_conv_w[layer], fpad)
    cb = _pad_cols(ffn_conv_b[layer], fpad)
    wd = ffn_w_down[layer]
    wd = jnp.pad(wd, ((0, (-wd.shape[0]) % fpad), (0, 0))).astype(BF16)
    a = _ffn_up(h, norm_pre_ffn[layer], wg, wu, cw, cb, seq_len)
    return _mm(a, wd, out_dtype=F32, post_gain=norm_post_ffn[layer], res=h, bk=512,
               name="ffn_down")


def kernel(x, positions, norm_pre_mix, norm_post_mix, norm_pre_ffn, norm_post_ffn, fox_w_qkvf, fox_b_f, fox_w_o, mla_w_dq, mla_q_norm, mla_w_uq, mla_w_o, kv_in_norm, mla_w_dkv, mla_kv_norm, mla_w_uk, mla_w_uv, ffn_w_gate, ffn_conv_w, ffn_conv_b, ffn_w_up, ffn_w_down):
    B, S, D = x.shape
    T = B * S
    H = D // LANES
    n_a = fox_w_qkvf.shape[0]
    n_b = mla_w_dq.shape[0]
    lat = mla_w_uk.shape[0]
    rope = mla_w_dkv.shape[1] - lat
    nope = mla_w_uk.shape[1] // H
    half = rope // 2
    assert nope == LANES and 2 * rope == LANES and mla_w_uv.shape[1] == D
    assert fox_w_qkvf.shape[2] == 3 * D + H and H <= LANES

    ffn_args = (norm_pre_ffn, norm_post_ffn, ffn_w_gate, ffn_conv_w, ffn_conv_b, ffn_w_up,
                ffn_w_down)
    h = x.reshape(T, D)

    for layer in range(n_a):
        w = fox_w_qkvf[layer]
        fox_scale = 1.0 / math.sqrt(LANES)
        wqkv = jnp.concatenate([w[:, :D] * fox_scale, w[:, D:3 * D]], axis=1).astype(BF16)
        wf = _pad_cols(w[:, 3 * D:], LANES).astype(BF16)
        bias = _pad_cols(fox_b_f[layer].reshape(1, H), LANES)
        gain = norm_pre_mix[layer]
        qkv = _mm(h, wqkv, out_dtype=BF16, pre_gain=gain, name="fox_qkv")
        flog = _mm(h, wf, out_dtype=F32, pre_gain=gain, name="fox_gate")
        c = _cumgate(flog, bias, B, S)
        neg = -c[:, :H].reshape(B, S, H).transpose(0, 2, 1)
        kb = _pad_cols(jnp.stack(_split_bf16(neg), axis=-1), LANES).reshape(B * H * S, LANES)
        o = _attention(qkv, LANES, qkv, H, kb, True, qkv, 2 * H, B, S, H,
                       ones_rows=FORGET_SPLITS)
        h = _mm(o, fox_w_o[layer].astype(BF16), out_dtype=F32,
                post_gain=norm_post_mix[layer], res=h, bk=512, name="fox_out")
        h = _ffn(h, layer, S, *ffn_args)

    if n_b:
        inv_freq = ROPE_THETA ** (-jnp.arange(0, rope, 2, dtype=F32) / rope)
        ang = positions.astype(F32).reshape(T, 1) * inv_freq[None, :]
        ang4 = jnp.tile(ang, (1, 4))
        one, zero = jnp.ones((half,), F32), jnp.zeros((half,), F32)
        sgn = jnp.concatenate([-one, one, -one, one]).reshape(1, LANES)
        selc = jnp.concatenate([one, one, zero, zero]).reshape(1, LANES)
        sels = jnp.concatenate([zero, zero, -one, one]).reshape(1, LANES)
        ta, tb, tq_rope = _rope_tables(ang4, sgn, selc, sels)
        mla_scale = 1.0 / math.sqrt(nope + rope)
        tq = mla_scale * jnp.concatenate([jnp.ones((T, nope), F32), tq_rope], axis=1)

        y1, y2 = mla_w_dkv[:, lat:lat + half], mla_w_dkv[:, lat + half:]
        wdkv = jnp.concatenate([mla_w_dkv[:, :lat], y1, y2, y1, y2, y2, y1, y2, y1],
                               axis=1).astype(BF16)
        ckv, kr = _dkv(h, kv_in_norm, wdkv, mla_kv_norm, ta, tb, lat)
        wkv = jnp.concatenate([mla_w_uk, mla_w_uv], axis=1).astype(BF16)
        kv = _mm(ckv, wkv, out_dtype=BF16, name="mla_kv_up")

        for jb in range(n_b):
            layer = n_a + jb
            cq = _mm(h, mla_w_dq[jb].astype(BF16), out_dtype=BF16,
                     pre_gain=norm_pre_mix[layer], post_gain=mla_q_norm[jb], name="mla_dq")
            wq = mla_w_uq[jb].reshape(-1, H, nope + rope)
            x1, x2 = wq[..., nope:nope + half], wq[..., nope + half:]
            wuq = jnp.concatenate([wq[..., :nope], x1, x2, x2, x1], axis=-1)
            wuq = wuq.reshape(-1, H * 2 * LANES).astype(BF16)
            q = _mm(cq, wuq, out_dtype=BF16, mul=tq, name="mla_uq")
            o = _attention(q, 2 * LANES, kv, 0, kr, False, kv, H, B, S, H, ones_rows=0)
            h = _mm(o, mla_w_o[jb].astype(BF16), out_dtype=F32,
                    post_gain=norm_post_mix[layer], res=h, bk=512, name="mla_out")
            h = _ffn(h, layer, S, *ffn_args)

    return h.reshape(B, S, D)
```

```python
import functools
import math

import jax
import jax.numpy as jnp
from jax import lax
from jax.experimental import pallas as pl
from jax.experimental.pallas import tpu as pltpu

RMS_EPS = 1e-6
ROPE_THETA = 10000.0
FORGET_SPLITS = 3
MASK_VALUE = -1e30
LOG2E = math.log2(math.e)

LANES = 128
SUBLANES = 8
V7X_VMEM_BYTES = 64 * 1024 * 1024
VMEM_RESERVE_BYTES = 6 * 1024 * 1024
MXU_DEPTH = 256

BF16 = jnp.bfloat16
F32 = jnp.float32


def _vmem_limit(block_bytes, scratch_bytes, temp_bytes):
    need = 2 * block_bytes + scratch_bytes + temp_bytes
    return int(min(max(need, 16 * 1024 * 1024), V7X_VMEM_BYTES - VMEM_RESERVE_BYTES))


def _pick(n, prefs):
    for p in prefs:
        if n % p == 0:
            return p
    return n


def _rms(y, gain):
    ms = jnp.mean(y * y, axis=-1, keepdims=True)
    return y * lax.rsqrt(ms + RMS_EPS) * gain


def _mm_kernel(*refs, nk, pre_norm, post_norm, has_res, mul_period, acc_in_out):
    it = iter(refs)
    x_ref = next(it)
    w_ref = next(it)
    pg_ref = next(it) if pre_norm else None
    og_ref = next(it) if post_norm else None
    res_ref = next(it) if has_res else None
    mul_ref = next(it) if mul_period else None
    o_ref = next(it)
    xn_ref = next(it) if pre_norm else None
    acc_ref = o_ref if acc_in_out else (next(it) if nk > 1 else None)

    j = pl.program_id(1)
    k = pl.program_id(2)

    if pre_norm:
        @pl.when(j == 0)
        def _():
            xn_ref[...] = _rms(x_ref[...], pg_ref[...]).astype(BF16)
        lhs = xn_ref[...]
    else:
        lhs = x_ref[...]

    if nk > 1:
        @pl.when(k == 0)
        def _():
            acc_ref[...] = jnp.zeros(acc_ref.shape, F32)

    part = jnp.dot(lhs, w_ref[...], preferred_element_type=F32)

    def epilogue(y):
        if post_norm:
            y = _rms(y, og_ref[...])
        if has_res:
            y = res_ref[...] + y
        if mul_period:
            for p in range(y.shape[1] // mul_period):
                sl = slice(p * mul_period, (p + 1) * mul_period)
                o_ref[:, sl] = (y[:, sl] * mul_ref[...]).astype(o_ref.dtype)
        else:
            o_ref[...] = y.astype(o_ref.dtype)

    if nk == 1:
        epilogue(part)
    else:
        acc_ref[...] += part

        @pl.when(k == nk - 1)
        def _():
            epilogue(acc_ref[...])


def _mm(x, w, *, out_dtype, pre_gain=None, post_gain=None, res=None, mul=None,
        bm=512, bn=None, bk=None, name="mm"):
    M, K = x.shape
    N = w.shape[1]
    bm = _pick(M, (bm, 256, 128, 64, 32, 16, 8))
    if post_gain is not None:
        bn = N
    elif bn is None:
        bn = _pick(N, (1024, 512, 256, 128))
    if pre_gain is not None or bk is None:
        bk = K
    else:
        bk = _pick(K, (bk, 256, 128))
    assert M % bm == 0 and N % bn == 0 and K % bk == 0
    nk = K // bk
    pre_norm, post_norm, has_res = pre_gain is not None, post_gain is not None, res is not None
    mul_period = 0 if mul is None else mul.shape[1]
    acc_in_out = nk > 1 and jnp.dtype(out_dtype) == jnp.dtype(F32)

    ins = [x, w]
    specs = [pl.BlockSpec((bm, bk), lambda i, j, k: (i, k)),
             pl.BlockSpec((bk, bn), lambda i, j, k: (k, j))]
    blk = bm * bk * x.dtype.itemsize + bk * bn * w.dtype.itemsize
    if pre_norm:
        ins.append(pre_gain.reshape(1, K).astype(F32))
        specs.append(pl.BlockSpec((1, K), lambda i, j, k: (0, 0)))
    if post_norm:
        ins.append(post_gain.reshape(1, N).astype(F32))
        specs.append(pl.BlockSpec((1, bn), lambda i, j, k: (0, j)))
    if has_res:
        ins.append(res)
        specs.append(pl.BlockSpec((bm, bn), lambda i, j, k: (i, j)))
        blk += bm * bn * res.dtype.itemsize
    if mul_period:
        assert bn % mul_period == 0
        ins.append(mul)
        specs.append(pl.BlockSpec((bm, mul_period), lambda i, j, k: (i, 0)))
        blk += bm * mul_period * 4
    blk += bm * bn * jnp.dtype(out_dtype).itemsize

    scratch, scratch_bytes = [], 0
    if pre_norm:
        scratch.append(pltpu.VMEM((bm, K), BF16))
        scratch_bytes += bm * K * 2
    if nk > 1 and not acc_in_out:
        scratch.append(pltpu.VMEM((bm, bn), F32))
        scratch_bytes += bm * bn * 4
    temp = 3 * bm * bn * 4 + (bm * K * 4 if pre_norm else 0)

    return pl.pallas_call(
        functools.partial(_mm_kernel, nk=nk, pre_norm=pre_norm, post_norm=post_norm,
                          has_res=has_res, mul_period=mul_period, acc_in_out=acc_in_out),
        grid=(M // bm, N // bn, nk),
        in_specs=specs,
        out_specs=pl.BlockSpec((bm, bn), lambda i, j, k: (i, j)),
        out_shape=jax.ShapeDtypeStruct((M, N), out_dtype),
        scratch_shapes=scratch,
        compiler_params=pltpu.CompilerParams(
            dimension_semantics=("arbitrary", "arbitrary", "arbitrary"),
            vmem_limit_bytes=_vmem_limit(blk, scratch_bytes, temp)),
        name=name,
    )(*ins)


def _ffn_up_kernel(h_ref, gain_ref, wg_ref, wu_ref, cw_ref, cb_ref, a_ref, xn_ref, tail_ref,
                   *, blocks_per_seq):
    i = pl.program_id(0)
    j = pl.program_id(1)

    @pl.when(j == 0)
    def _():
        xn_ref[...] = _rms(h_ref[...], gain_ref[...]).astype(BF16)

    @pl.when(i % blocks_per_seq == 0)
    def _():
        tail_ref[j] = jnp.zeros(tail_ref.shape[1:], F32)

    xn = xn_ref[...]
    g = jnp.dot(xn, wg_ref[...], preferred_element_type=F32)
    u = jnp.dot(xn, wu_ref[...], preferred_element_type=F32)
    bm = g.shape[0]
    prev = tail_ref[j]
    tail_ref[j] = g[bm - SUBLANES:, :]
    row = lax.broadcasted_iota(jnp.int32, g.shape, 0)
    g1 = jnp.where(row == 0, prev[SUBLANES - 1:SUBLANES, :], pltpu.roll(g, 1, axis=0))
    g2 = jnp.where(row == 0, prev[SUBLANES - 2:SUBLANES - 1, :],
                   jnp.where(row == 1, prev[SUBLANES - 1:SUBLANES, :], pltpu.roll(g, 2, axis=0)))
    cw = cw_ref[...]
    gc = g2 * cw[0:1, :] + g1 * cw[1:2, :] + g * cw[2:3, :] + cb_ref[...]
    a_ref[...] = (gc * jax.nn.sigmoid(gc) * u).astype(a_ref.dtype)


def _ffn_up(h, gain, wg, wu, conv_w, conv_b, seq_len, *, bm=512, fc=512):
    T, D = h.shape
    Fp = wg.shape[1]
    bm = _pick(seq_len, (bm, 256, 128, 64, 32, 16, 8))
    fc = _pick(Fp, (fc, 256, 128))
    nf = Fp // fc
    blk = bm * D * 4 + 2 * D * fc * 2 + 4 * fc * 4 + bm * fc * 2
    scratch_bytes = bm * D * 2 + nf * SUBLANES * fc * 4
    temp = bm * D * 4 + 8 * bm * fc * 4
    return pl.pallas_call(
        functools.partial(_ffn_up_kernel, blocks_per_seq=seq_len // bm),
        grid=(T // bm, nf),
        in_specs=[pl.BlockSpec((bm, D), lambda i, j: (i, 0)),
                  pl.BlockSpec((1, D), lambda i, j: (0, 0)),
                  pl.BlockSpec((D, fc), lambda i, j: (0, j)),
                  pl.BlockSpec((D, fc), lambda i, j: (0, j)),
                  pl.BlockSpec((3, fc), lambda i, j: (0, j)),
                  pl.BlockSpec((1, fc), lambda i, j: (0, j))],
        out_specs=pl.BlockSpec((bm, fc), lambda i, j: (i, j)),
        out_shape=jax.ShapeDtypeStruct((T, Fp), BF16),
        scratch_shapes=[pltpu.VMEM((bm, D), BF16), pltpu.VMEM((nf, SUBLANES, fc), F32)],
        compiler_params=pltpu.CompilerParams(
            dimension_semantics=("arbitrary", "arbitrary"),
            vmem_limit_bytes=_vmem_limit(blk, scratch_bytes, temp)),
        name="ffn_up",
    )(h, gain.reshape(1, D).astype(F32), wg, wu, conv_w, conv_b.reshape(1, Fp))


def _cumgate_kernel(f_ref, b_ref, c_ref, carry_ref):
    @pl.when(pl.program_id(1) == 0)
    def _():
        carry_ref[...] = jnp.zeros(carry_ref.shape, F32)

    x = f_ref[...] + b_ref[...]
    lf = jnp.minimum(x, 0.0) - jnp.log1p(jnp.exp(-jnp.abs(x)))
    rt = lf.shape[0]
    tri = (lax.broadcasted_iota(jnp.int32, (rt, rt), 0)
           >= lax.broadcasted_iota(jnp.int32, (rt, rt), 1)).astype(BF16)
    y = carry_ref[0:1, :]
    rem = lf
    for _ in range(FORGET_SPLITS):
        piece = rem.astype(BF16)
        y = y + jnp.dot(tri, piece, preferred_element_type=F32)
        rem = rem - piece.astype(F32)
    c_ref[...] = y
    carry_ref[0:1, :] = y[rt - 1:rt, :]


def _cumgate(flog, bias, batch, seq_len):
    T, W = flog.shape
    rt = _pick(seq_len, (256, 128, 64, 32, 16, 8))
    nt = seq_len // rt
    return pl.pallas_call(
        _cumgate_kernel,
        grid=(batch, nt),
        in_specs=[pl.BlockSpec((rt, W), lambda b, t: (b * nt + t, 0)),
                  pl.BlockSpec((1, W), lambda b, t: (0, 0))],
        out_specs=pl.BlockSpec((rt, W), lambda b, t: (b * nt + t, 0)),
        out_shape=jax.ShapeDtypeStruct((T, W), F32),
        scratch_shapes=[pltpu.VMEM((SUBLANES, W), F32)],
        compiler_params=pltpu.CompilerParams(dimension_semantics=("arbitrary", "arbitrary")),
        name="cumgate",
    )(flog, bias)


def _rope_tables_kernel(ang_ref, sgn_ref, selc_ref, sels_ref, ta_ref, tb_ref, tq_ref):
    c = jnp.cos(ang_ref[...])
    s = jnp.sin(ang_ref[...])
    ta_ref[...] = c
    tb_ref[...] = s * sgn_ref[...]
    tq_ref[...] = c * selc_ref[...] + s * sels_ref[...]


def _rope_tables(ang4, sgn, selc, sels):
    T, W = ang4.shape
    bt = _pick(T, (1024, 512, 256, 128, 64, 32, 16, 8))
    row = pl.BlockSpec((bt, W), lambda i: (i, 0))
    const = pl.BlockSpec((1, W), lambda i: (0, 0))
    return pl.pallas_call(
        _rope_tables_kernel,
        grid=(T // bt,),
        in_specs=[row, const, const, const],
        out_specs=[row, row, row],
        out_shape=[jax.ShapeDtypeStruct((T, W), F32)] * 3,
        compiler_params=pltpu.CompilerParams(dimension_semantics=("arbitrary",)),
        name="rope_tables",
    )(ang4, sgn, selc, sels)


def _dkv_kernel(h_ref, gin_ref, w_ref, gkv_ref, ta_ref, tb_ref, ckv_ref, kr_ref, *, lat):
    xn = _rms(h_ref[...], gin_ref[...]).astype(BF16)
    y = jnp.dot(xn, w_ref[...], preferred_element_type=F32)
    ckv_ref[...] = _rms(y[:, :lat], gkv_ref[...]).astype(ckv_ref.dtype)
    kr = y[:, lat:lat + LANES] * ta_ref[...] + y[:, lat + LANES:lat + 2 * LANES] * tb_ref[...]
    kr_ref[...] = kr.astype(kr_ref.dtype)


def _dkv(h, gin, w, gkv, ta, tb, lat, *, bm=512):
    T, D = h.shape
    Nw = w.shape[1]
    bm = _pick(T, (bm, 256, 128, 64, 32, 16, 8))
    blk = bm * D * 4 + D * Nw * 2 + 2 * bm * LANES * 4 + bm * (lat + LANES) * 2
    temp = bm * D * 6 + 2 * bm * Nw * 4
    return pl.pallas_call(
        functools.partial(_dkv_kernel, lat=lat),
        grid=(T // bm,),
        in_specs=[pl.BlockSpec((bm, D), lambda i: (i, 0)),
                  pl.BlockSpec((1, D), lambda i: (0, 0)),
                  pl.BlockSpec((D, Nw), lambda i: (0, 0)),
                  pl.BlockSpec((1, lat), lambda i: (0, 0)),
                  pl.BlockSpec((bm, LANES), lambda i: (i, 0)),
                  pl.BlockSpec((bm, LANES), lambda i: (i, 0))],
        out_specs=[pl.BlockSpec((bm, lat), lambda i: (i, 0)),
                   pl.BlockSpec((bm, LANES), lambda i: (i, 0))],
        out_shape=[jax.ShapeDtypeStruct((T, lat), BF16), jax.ShapeDtypeStruct((T, LANES), BF16)],
        compiler_params=pltpu.CompilerParams(
            dimension_semantics=("arbitrary",),
            vmem_limit_bytes=_vmem_limit(blk, 0, temp)),
        name="mla_dkv",
    )(h, gin.reshape(1, D).astype(F32), w, gkv.reshape(1, lat).astype(F32), ta, tb)


def _attn_kernel(q_ref, ka_ref, kb_ref, v_ref, o_ref, kcat, vt, qt, s_a, s_b, m_s, l_s, acc_s,
                 *, ones_rows):
    qi = pl.program_id(2)
    nch, ch, _ = kcat.shape
    bq = q_ref.shape[0]
    qw = q_ref.shape[1]

    @pl.when(qi == 0)
    def _():
        def build(c, carry):
            r = pl.multiple_of(c * ch, ch)
            kcat[c, :, 0:LANES] = ka_ref[pl.ds(r, ch), :]
            kcat[c, :, LANES:2 * LANES] = kb_ref[pl.ds(r, ch), :]
            vt[c] = v_ref[pl.ds(r, ch), :].astype(F32).T.astype(BF16)
            return carry
        lax.fori_loop(0, nch, build, 0)

    qt[0:qw, :] = q_ref[...].astype(F32).T.astype(BF16)
    if ones_rows:
        rowi = lax.broadcasted_iota(jnp.int32, (LANES, bq), 0)
        qt[LANES:2 * LANES, :] = jnp.where(rowi < ones_rows, 1.0, 0.0).astype(BF16)
    m_s[...] = jnp.full(m_s.shape, MASK_VALUE, F32)
    l_s[...] = jnp.zeros(l_s.shape, F32)
    acc_s[...] = jnp.zeros(acc_s.shape, F32)

    def scores(c, dst):
        dst[...] = jnp.dot(kcat[c], qt[...], preferred_element_type=F32)

    def softmax_pv(src, c, masked):
        s = src[...]
        if masked:
            key = lax.broadcasted_iota(jnp.int32, s.shape, 0)
            qry = lax.broadcasted_iota(jnp.int32, s.shape, 1)
            s = jnp.where(key <= qry, s, MASK_VALUE)
        m_prev = m_s[...]
        m_new = jnp.maximum(m_prev, jnp.max(s, axis=0, keepdims=True))
        alpha = jnp.exp2(m_prev - m_new)
        p = jnp.exp2(s - m_new)
        l_s[...] = alpha * l_s[...] + jnp.sum(p, axis=0, keepdims=True)
        acc_s[...] = alpha * acc_s[...] + jnp.dot(vt[c], p.astype(BF16),
                                                  preferred_element_type=F32)
        m_s[...] = m_new

    scores(0, s_a)

    def pair(p, carry):
        c = 2 * p
        scores(c + 1, s_b)
        softmax_pv(s_a, c, False)
        scores(c + 2, s_a)
        softmax_pv(s_b, c + 1, False)
        return carry
    lax.fori_loop(0, qi // 2, pair, 0)

    @pl.when(qi % 2 == 0)
    def _():
        softmax_pv(s_a, qi, True)

    @pl.when(qi % 2 == 1)
    def _():
        scores(qi, s_b)
        softmax_pv(s_a, qi - 1, False)
        softmax_pv(s_b, qi, True)

    o_ref[...] = (acc_s[...] / l_s[...]).T.astype(o_ref.dtype)


def _attention(q_arr, q_cols, ka_arr, ka_col0, kb_arr, kb_per_head, v_arr, v_col0,
               batch, seq_len, heads, *, ones_rows, bq=512):
    T = batch * seq_len
    bq = _pick(seq_len, (bq, 256, 128))
    nq = seq_len // bq
    blk = bq * q_cols * 2 + 3 * seq_len * LANES * 2 + bq * LANES * 2
    scratch_bytes = seq_len * 2 * LANES * 2 + seq_len * LANES * 2 + 2 * LANES * bq * 2 \
        + 2 * SUBLANES * bq * 4 + LANES * bq * 4 + 2 * bq * bq * 4
    temp = 6 * bq * bq * 4
    kb_map = (lambda b, h, i: (b * heads + h, 0)) if kb_per_head else (lambda b, h, i: (b, 0))
    return pl.pallas_call(
        functools.partial(_attn_kernel, ones_rows=ones_rows),
        grid=(batch, heads, nq),
        in_specs=[pl.BlockSpec((bq, q_cols), lambda b, h, i: (b * nq + i, h)),
                  pl.BlockSpec((seq_len, LANES), lambda b, h, i: (b, ka_col0 + h)),
                  pl.BlockSpec((seq_len, LANES), kb_map),
                  pl.BlockSpec((seq_len, LANES), lambda b, h, i: (b, v_col0 + h))],
        out_specs=pl.BlockSpec((bq, LANES), lambda b, h, i: (b * nq + i, h)),
        out_shape=jax.ShapeDtypeStruct((T, heads * LANES), BF16),
        scratch_shapes=[pltpu.VMEM((nq, bq, 2 * LANES), BF16),
                        pltpu.VMEM((nq, LANES, bq), BF16),
                        pltpu.VMEM((2 * LANES, bq), BF16),
                        pltpu.VMEM((bq, bq), F32),
                        pltpu.VMEM((bq, bq), F32),
                        pltpu.VMEM((1, bq), F32),
                        pltpu.VMEM((1, bq), F32),
                        pltpu.VMEM((LANES, bq), F32)],
        compiler_params=pltpu.CompilerParams(
            dimension_semantics=("arbitrary", "arbitrary", "arbitrary"),
            vmem_limit_bytes=_vmem_limit(blk, scratch_bytes, temp)),
        name="attention",
    )(q_arr, ka_arr, kb_arr, v_arr)


def _split_bf16(x):
    pieces, rem = [], x
    for _ in range(FORGET_SPLITS - 1):
        t = rem * (2.0 ** 16 + 1.0)
        hi = t - (t - rem)
        pieces.append(hi.astype(BF16))
        rem = rem - hi
    pieces.append(rem.astype(BF16))
    return pieces


def _pad_cols(w, mult):
    pad = (-w.shape[-1]) % mult
    return w if pad == 0 else jnp.pad(w, [(0, 0)] * (w.ndim - 1) + [(0, pad)])


def _ffn(h, layer, seq_len, norm_pre_ffn, norm_post_ffn, ffn_w_gate, ffn_conv_w, ffn_conv_b,
         ffn_w_up, ffn_w_down):
    fpad = 1024
    wg = _pad_cols(ffn_w_gate[layer], fpad).astype(BF16)
    wu = _pad_cols(ffn_w_up[layer], fpad).astype(BF16)
    cw = _pad_cols(ffn_conv_w[layer], fpad)
    cb = _pad_cols(ffn_conv_b[layer], fpad)
    wd = ffn_w_down[layer]
    wd = jnp.pad(wd, ((0, (-wd.shape[0]) % fpad), (0, 0))).astype(BF16)
    a = _ffn_up(h, norm_pre_ffn[layer], wg, wu, cw, cb, seq_len)
    return _mm(a, wd, out_dtype=F32, post_gain=norm_post_ffn[layer], res=h, bk=512,
               name="ffn_down")


def kernel(x, positions, norm_pre_mix, norm_post_mix, norm_pre_ffn, norm_post_ffn, fox_w_qkvf, fox_b_f, fox_w_o, mla_w_dq, mla_q_norm, mla_w_uq, mla_w_o, kv_in_norm, mla_w_dkv, mla_kv_norm, mla_w_uk, mla_w_uv, ffn_w_gate, ffn_conv_w, ffn_conv_b, ffn_w_up, ffn_w_down):
    B, S, D = x.shape
    T = B * S
    H = D // LANES
    n_a = fox_w_qkvf.shape[0]
    n_b = mla_w_dq.shape[0]
    lat = mla_w_uk.shape[0]
    rope = mla_w_dkv.shape[1] - lat
    nope = mla_w_uk.shape[1] // H
    half = rope // 2
    assert nope == LANES and 2 * rope == LANES and mla_w_uv.shape[1] == D
    assert fox_w_qkvf.shape[2] == 3 * D + H and H <= LANES

    ffn_args = (norm_pre_ffn, norm_post_ffn, ffn_w_gate, ffn_conv_w, ffn_conv_b, ffn_w_up,
                ffn_w_down)
    h = x.reshape(T, D)

    for layer in range(n_a):
        w = fox_w_qkvf[layer]
        fox_scale = LOG2E / math.sqrt(LANES)
        wqkv = jnp.concatenate([w[:, :D] * fox_scale, w[:, D:3 * D]], axis=1).astype(BF16)
        wf = _pad_cols(w[:, 3 * D:], LANES).astype(BF16)
        bias = _pad_cols(fox_b_f[layer].reshape(1, H), LANES)
        gain = norm_pre_mix[layer]
        qkv = _mm(h, wqkv, out_dtype=BF16, pre_gain=gain, name="fox_qkv")
        flog = _mm(h, wf, out_dtype=F32, pre_gain=gain, name="fox_gate")
        c = _cumgate(flog, bias, B, S)
        neg = (-LOG2E) * c[:, :H].reshape(B, S, H).transpose(0, 2, 1)
        kb = _pad_cols(jnp.stack(_split_bf16(neg), axis=-1), LANES).reshape(B * H * S, LANES)
        o = _attention(qkv, LANES, qkv, H, kb, True, qkv, 2 * H, B, S, H,
                       ones_rows=FORGET_SPLITS)
        h = _mm(o, fox_w_o[layer].astype(BF16), out_dtype=F32,
                post_gain=norm_post_mix[layer], res=h, bk=512, name="fox_out")
        h = _ffn(h, layer, S, *ffn_args)

    if n_b:
        inv_freq = ROPE_THETA ** (-jnp.arange(0, rope, 2, dtype=F32) / rope)
        ang = positions.astype(F32).reshape(T, 1) * inv_freq[None, :]
        ang4 = jnp.tile(ang, (1, 4))
        one, zero = jnp.ones((half,), F32), jnp.zeros((half,), F32)
        sgn = jnp.concatenate([-one, one, -one, one]).reshape(1, LANES)
        selc = jnp.concatenate([one, one, zero, zero]).reshape(1, LANES)
        sels = jnp.concatenate([zero, zero, -one, one]).reshape(1, LANES)
        ta, tb, tq_rope = _rope_tables(ang4, sgn, selc, sels)
        mla_scale = LOG2E / math.sqrt(nope + rope)
        tq = mla_scale * jnp.concatenate([jnp.ones((T, nope), F32), tq_rope], axis=1)

        y1, y2 = mla_w_dkv[:, lat:lat + half], mla_w_dkv[:, lat + half:]
        wdkv = jnp.concatenate([mla_w_dkv[:, :lat], y1, y2, y1, y2, y2, y1, y2, y1],
                               axis=1).astype(BF16)
        ckv, kr = _dkv(h, kv_in_norm, wdkv, mla_kv_norm, ta, tb, lat)
        wkv = jnp.concatenate([mla_w_uk, mla_w_uv], axis=1).astype(BF16)
        kv = _mm(ckv, wkv, out_dtype=BF16, name="mla_kv_up")

        for jb in range(n_b):
            layer = n_a + jb
            cq = _mm(h, mla_w_dq[jb].astype(BF16), out_dtype=BF16,
                     pre_gain=norm_pre_mix[layer], post_gain=mla_q_norm[jb], name="mla_dq")
            wq = mla_w_uq[jb].reshape(-1, H, nope + rope)
            x1, x2 = wq[..., nope:nope + half], wq[..., nope + half:]
            wuq = jnp.concatenate([wq[..., :nope], x1, x2, x2, x1], axis=-1)
            wuq = wuq.reshape(-1, H * 2 * LANES).astype(BF16)
            q = _mm(cq, wuq, out_dtype=BF16, mul=tq, name="mla_uq")
            o = _attention(q, 2 * LANES, kv, 0, kr, False, kv, H, B, S, H, ones_rows=0)
            h = _mm(o, mla_w_o[jb].astype(BF16), out_dtype=F32,
                    post_gain=norm_post_mix[layer], res=h, bk=512, name="mla_out")
            h = _ffn(h, layer, S, *ffn_args)

    return h.reshape(B, S, D)
```

```python
import functools
import math

import jax
import jax.numpy as jnp
from jax import lax
from jax.experimental import pallas as pl
from jax.experimental.pallas import tpu as pltpu

RMS_EPS = 1e-6
ROPE_THETA = 10000.0
FORGET_SPLITS = 3
MASK_VALUE = -1e30
LOG2E = math.log2(math.e)

LANES = 128
SUBLANES = 8
BF16_ROWS = 16
V7X_VMEM_BYTES = 64 * 1024 * 1024
VMEM_RESERVE_BYTES = 6 * 1024 * 1024
MXU_DEPTH = 256

BF16 = jnp.bfloat16
F32 = jnp.float32


def _vmem_limit(block_bytes, scratch_bytes, temp_bytes):
    need = 2 * block_bytes + scratch_bytes + temp_bytes
    return int(min(max(need, 16 * 1024 * 1024), V7X_VMEM_BYTES - VMEM_RESERVE_BYTES))


def _pick(n, prefs):
    for p in prefs:
        if n % p == 0:
            return p
    return n


def _rms(y, gain):
    ms = jnp.mean(y * y, axis=-1, keepdims=True)
    return y * lax.rsqrt(ms + RMS_EPS) * gain


def _mm_kernel(*refs, nk, pre_norm, post_norm, has_res, mul_period, acc_in_out):
    it = iter(refs)
    x_ref = next(it)
    w_ref = next(it)
    pg_ref = next(it) if pre_norm else None
    og_ref = next(it) if post_norm else None
    res_ref = next(it) if has_res else None
    mul_ref = next(it) if mul_period else None
    o_ref = next(it)
    xn_ref = next(it) if pre_norm else None
    acc_ref = o_ref if acc_in_out else (next(it) if nk > 1 else None)

    j = pl.program_id(1)
    k = pl.program_id(2)

    if pre_norm:
        @pl.when(j == 0)
        def _():
            xn_ref[...] = _rms(x_ref[...], pg_ref[...]).astype(BF16)
        lhs = xn_ref[...]
    else:
        lhs = x_ref[...]

    if nk > 1:
        @pl.when(k == 0)
        def _():
            acc_ref[...] = jnp.zeros(acc_ref.shape, F32)

    part = jnp.dot(lhs, w_ref[...], preferred_element_type=F32)

    def epilogue(y):
        if post_norm:
            y = _rms(y, og_ref[...])
        if has_res:
            y = res_ref[...] + y
        if mul_period:
            for p in range(y.shape[1] // mul_period):
                sl = slice(p * mul_period, (p + 1) * mul_period)
                o_ref[:, sl] = (y[:, sl] * mul_ref[...]).astype(o_ref.dtype)
        else:
            o_ref[...] = y.astype(o_ref.dtype)

    if nk == 1:
        epilogue(part)
    else:
        acc_ref[...] += part

        @pl.when(k == nk - 1)
        def _():
            epilogue(acc_ref[...])


def _mm(x, w, *, out_dtype, pre_gain=None, post_gain=None, res=None, mul=None,
        bm=512, bn=None, bk=None, name="mm"):
    M, K = x.shape
    N = w.shape[1]
    bm = _pick(M, (bm, 256, 128, 64, 32, 16, 8))
    if post_gain is not None:
        bn = N
    elif bn is None:
        bn = _pick(N, (1024, 512, 256, 128))
    if pre_gain is not None or bk is None:
        bk = K
    else:
        bk = _pick(K, (bk, 256, 128))
    assert M % bm == 0 and N % bn == 0 and K % bk == 0
    nk = K // bk
    pre_norm, post_norm, has_res = pre_gain is not None, post_gain is not None, res is not None
    mul_period = 0 if mul is None else mul.shape[1]
    acc_in_out = nk > 1 and jnp.dtype(out_dtype) == jnp.dtype(F32)

    ins = [x, w]
    specs = [pl.BlockSpec((bm, bk), lambda i, j, k: (i, k)),
             pl.BlockSpec((bk, bn), lambda i, j, k: (k, j))]
    blk = bm * bk * x.dtype.itemsize + bk * bn * w.dtype.itemsize
    if pre_norm:
        ins.append(pre_gain.reshape(1, K).astype(F32))
        specs.append(pl.BlockSpec((1, K), lambda i, j, k: (0, 0)))
    if post_norm:
        ins.append(post_gain.reshape(1, N).astype(F32))
        specs.append(pl.BlockSpec((1, bn), lambda i, j, k: (0, j)))
    if has_res:
        ins.append(res)
        specs.append(pl.BlockSpec((bm, bn), lambda i, j, k: (i, j)))
        blk += bm * bn * res.dtype.itemsize
    if mul_period:
        assert bn % mul_period == 0
        ins.append(mul)
        specs.append(pl.BlockSpec((bm, mul_period), lambda i, j, k: (i, 0)))
        blk += bm * mul_period * 4
    blk += bm * bn * jnp.dtype(out_dtype).itemsize

    scratch, scratch_bytes = [], 0
    if pre_norm:
        scratch.append(pltpu.VMEM((bm, K), BF16))
        scratch_bytes += bm * K * 2
    if nk > 1 and not acc_in_out:
        scratch.append(pltpu.VMEM((bm, bn), F32))
        scratch_bytes += bm * bn * 4
    temp = 3 * bm * bn * 4 + (bm * K * 4 if pre_norm else 0)

    return pl.pallas_call(
        functools.partial(_mm_kernel, nk=nk, pre_norm=pre_norm, post_norm=post_norm,
                          has_res=has_res, mul_period=mul_period, acc_in_out=acc_in_out),
        grid=(M // bm, N // bn, nk),
        in_specs=specs,
        out_specs=pl.BlockSpec((bm, bn), lambda i, j, k: (i, j)),
        out_shape=jax.ShapeDtypeStruct((M, N), out_dtype),
        scratch_shapes=scratch,
        compiler_params=pltpu.CompilerParams(
            dimension_semantics=("arbitrary", "arbitrary", "arbitrary"),
            vmem_limit_bytes=_vmem_limit(blk, scratch_bytes, temp)),
        name=name,
    )(*ins)


def _cast_kernel(*refs, c_in, n_valid, has_scale):
    w_ref = refs[0]
    scale_ref = refs[1] if has_scale else None
    o_ref = refs[-1]
    c_out = o_ref.shape[1]

    @pl.when(pl.program_id(0) < n_valid)
    def _():
        x = w_ref[...]
        if has_scale:
            x = x * scale_ref[...]
        o_ref[:, 0:c_in] = x.astype(o_ref.dtype)
        if c_out > c_in:
            o_ref[:, c_in:c_out] = jnp.zeros((o_ref.shape[0], c_out - c_in), o_ref.dtype)

    @pl.when(pl.program_id(0) >= n_valid)
    def _():
        o_ref[...] = jnp.zeros(o_ref.shape, o_ref.dtype)


def _cast_pad(w, layer, *, c_in=None, r_out=None, c_out=None, scale=None, name="cast"):
    _, R, C = w.shape
    c_in = C if c_in is None else c_in
    r_out = R if r_out is None else r_out
    c_out = c_in if c_out is None else c_out
    assert c_in % LANES == 0 and c_out % LANES == 0
    target_rows = max(SUBLANES, (4 * 1024 * 1024) // (4 * c_in))
    br = _pick(math.gcd(R, r_out), tuple(b for b in (1024, 512, 256, 128, 64, 32, 16)
                                          if b <= target_rows))
    n_valid = R // br
    ins = [w]
    specs = [pl.BlockSpec((None, br, c_in), lambda i: (layer, jnp.minimum(i, n_valid - 1), 0))]
    if scale is not None:
        ins.append(scale.reshape(1, c_in).astype(F32))
        specs.append(pl.BlockSpec((1, c_in), lambda i: (0, 0)))
    blk = br * c_in * 4 + br * c_out * 2
    return pl.pallas_call(
        functools.partial(_cast_kernel, c_in=c_in, n_valid=n_valid, has_scale=scale is not None),
        grid=(r_out // br,),
        in_specs=specs,
        out_specs=pl.BlockSpec((br, c_out), lambda i: (i, 0)),
        out_shape=jax.ShapeDtypeStruct((r_out, c_out), BF16),
        compiler_params=pltpu.CompilerParams(
            dimension_semantics=("arbitrary",),
            vmem_limit_bytes=_vmem_limit(blk, 0, br * c_in * 4)),
        name=name,
    )(*ins)


def _ffn_up_kernel(h_ref, gain_ref, wg_ref, wu_ref, cw_ref, cb_ref, a_ref, xn_ref, tail_ref,
                   *, blocks_per_seq):
    i = pl.program_id(0)
    j = pl.program_id(1)

    @pl.when(j == 0)
    def _():
        xn_ref[...] = _rms(h_ref[...], gain_ref[...]).astype(BF16)

    @pl.when(i % blocks_per_seq == 0)
    def _():
        tail_ref[j] = jnp.zeros(tail_ref.shape[1:], F32)

    xn = xn_ref[...]
    g = jnp.dot(xn, wg_ref[...], preferred_element_type=F32)
    u = jnp.dot(xn, wu_ref[...], preferred_element_type=F32)
    bm = g.shape[0]
    prev = tail_ref[j]
    tail_ref[j] = g[bm - SUBLANES:, :]
    row = lax.broadcasted_iota(jnp.int32, g.shape, 0)
    g1 = jnp.where(row == 0, prev[SUBLANES - 1:SUBLANES, :], pltpu.roll(g, 1, axis=0))
    g2 = jnp.where(row == 0, prev[SUBLANES - 2:SUBLANES - 1, :],
                   jnp.where(row == 1, prev[SUBLANES - 1:SUBLANES, :], pltpu.roll(g, 2, axis=0)))
    cw = cw_ref[...]
    gc = g2 * cw[0:1, :] + g1 * cw[1:2, :] + g * cw[2:3, :] + cb_ref[...]
    a_ref[...] = (gc * jax.nn.sigmoid(gc) * u).astype(a_ref.dtype)


def _ffn_up(h, gain, wg, wu, conv_w, conv_b, seq_len, *, bm=512, fc=512):
    T, D = h.shape
    Fp = wg.shape[1]
    bm = _pick(seq_len, (bm, 256, 128, 64, 32, 16, 8))
    fc = _pick(Fp, (fc, 256, 128))
    nf = Fp // fc
    blk = bm * D * 4 + 2 * D * fc * 2 + 4 * fc * 4 + bm * fc * 2
    scratch_bytes = bm * D * 2 + nf * SUBLANES * fc * 4
    temp = bm * D * 4 + 8 * bm * fc * 4
    return pl.pallas_call(
        functools.partial(_ffn_up_kernel, blocks_per_seq=seq_len // bm),
        grid=(T // bm, nf),
        in_specs=[pl.BlockSpec((bm, D), lambda i, j: (i, 0)),
                  pl.BlockSpec((1, D), lambda i, j: (0, 0)),
                  pl.BlockSpec((D, fc), lambda i, j: (0, j)),
                  pl.BlockSpec((D, fc), lambda i, j: (0, j)),
                  pl.BlockSpec((3, fc), lambda i, j: (0, j)),
                  pl.BlockSpec((1, fc), lambda i, j: (0, j))],
        out_specs=pl.BlockSpec((bm, fc), lambda i, j: (i, j)),
        out_shape=jax.ShapeDtypeStruct((T, Fp), BF16),
        scratch_shapes=[pltpu.VMEM((bm, D), BF16), pltpu.VMEM((nf, SUBLANES, fc), F32)],
        compiler_params=pltpu.CompilerParams(
            dimension_semantics=("arbitrary", "arbitrary"),
            vmem_limit_bytes=_vmem_limit(blk, scratch_bytes, temp)),
        name="ffn_up",
    )(h, gain.reshape(1, D).astype(F32), wg, wu, conv_w, conv_b.reshape(1, Fp))


def _cumgate_kernel(f_ref, b_ref, c_ref, carry_ref):
    @pl.when(pl.program_id(1) == 0)
    def _():
        carry_ref[...] = jnp.zeros(carry_ref.shape, F32)

    x = f_ref[...] + b_ref[...]
    lf = jnp.minimum(x, 0.0) - jnp.log1p(jnp.exp(-jnp.abs(x)))
    rt = lf.shape[0]
    tri = (lax.broadcasted_iota(jnp.int32, (rt, rt), 0)
           >= lax.broadcasted_iota(jnp.int32, (rt, rt), 1)).astype(BF16)
    y = carry_ref[0:1, :]
    rem = lf
    for _ in range(FORGET_SPLITS):
        piece = rem.astype(BF16)
        y = y + jnp.dot(tri, piece, preferred_element_type=F32)
        rem = rem - piece.astype(F32)
    c_ref[...] = y
    carry_ref[0:1, :] = y[rt - 1:rt, :]


def _cumgate(flog, bias, batch, seq_len):
    T, W = flog.shape
    rt = _pick(seq_len, (256, 128, 64, 32, 16, 8))
    nt = seq_len // rt
    return pl.pallas_call(
        _cumgate_kernel,
        grid=(batch, nt),
        in_specs=[pl.BlockSpec((rt, W), lambda b, t: (b * nt + t, 0)),
                  pl.BlockSpec((1, W), lambda b, t: (0, 0))],
        out_specs=pl.BlockSpec((rt, W), lambda b, t: (b * nt + t, 0)),
        out_shape=jax.ShapeDtypeStruct((T, W), F32),
        scratch_shapes=[pltpu.VMEM((SUBLANES, W), F32)],
        compiler_params=pltpu.CompilerParams(dimension_semantics=("arbitrary", "arbitrary")),
        name="cumgate",
    )(flog, bias)


def _rope_tables_kernel(ang_ref, sgn_ref, selc_ref, sels_ref, ta_ref, tb_ref, tq_ref):
    c = jnp.cos(ang_ref[...])
    s = jnp.sin(ang_ref[...])
    ta_ref[...] = c
    tb_ref[...] = s * sgn_ref[...]
    tq_ref[...] = c * selc_ref[...] + s * sels_ref[...]


def _rope_tables(ang4, sgn, selc, sels):
    T, W = ang4.shape
    bt = _pick(T, (1024, 512, 256, 128, 64, 32, 16, 8))
    row = pl.BlockSpec((bt, W), lambda i: (i, 0))
    const = pl.BlockSpec((1, W), lambda i: (0, 0))
    return pl.pallas_call(
        _rope_tables_kernel,
        grid=(T // bt,),
        in_specs=[row, const, const, const],
        out_specs=[row, row, row],
        out_shape=[jax.ShapeDtypeStruct((T, W), F32)] * 3,
        compiler_params=pltpu.CompilerParams(dimension_semantics=("arbitrary",)),
        name="rope_tables",
    )(ang4, sgn, selc, sels)


def _dkv_kernel(h_ref, gin_ref, w_ref, gkv_ref, ta_ref, tb_ref, ckv_ref, kr_ref, *, lat):
    xn = _rms(h_ref[...], gin_ref[...]).astype(BF16)
    y = jnp.dot(xn, w_ref[...], preferred_element_type=F32)
    ckv_ref[...] = _rms(y[:, :lat], gkv_ref[...]).astype(ckv_ref.dtype)
    kr = y[:, lat:lat + LANES] * ta_ref[...] + y[:, lat + LANES:lat + 2 * LANES] * tb_ref[...]
    kr_ref[...] = kr.astype(kr_ref.dtype)


def _dkv(h, gin, w, gkv, ta, tb, lat, *, bm=512):
    T, D = h.shape
    Nw = w.shape[1]
    bm = _pick(T, (bm, 256, 128, 64, 32, 16, 8))
    blk = bm * D * 4 + D * Nw * 2 + 2 * bm * LANES * 4 + bm * (lat + LANES) * 2
    temp = bm * D * 6 + 2 * bm * Nw * 4
    return pl.pallas_call(
        functools.partial(_dkv_kernel, lat=lat),
        grid=(T // bm,),
        in_specs=[pl.BlockSpec((bm, D), lambda i: (i, 0)),
                  pl.BlockSpec((1, D), lambda i: (0, 0)),
                  pl.BlockSpec((D, Nw), lambda i: (0, 0)),
                  pl.BlockSpec((1, lat), lambda i: (0, 0)),
                  pl.BlockSpec((bm, LANES), lambda i: (i, 0)),
                  pl.BlockSpec((bm, LANES), lambda i: (i, 0))],
        out_specs=[pl.BlockSpec((bm, lat), lambda i: (i, 0)),
                   pl.BlockSpec((bm, LANES), lambda i: (i, 0))],
        out_shape=[jax.ShapeDtypeStruct((T, lat), BF16), jax.ShapeDtypeStruct((T, LANES), BF16)],
        compiler_params=pltpu.CompilerParams(
            dimension_semantics=("arbitrary",),
            vmem_limit_bytes=_vmem_limit(blk, 0, temp)),
        name="mla_dkv",
    )(h, gin.reshape(1, D).astype(F32), w, gkv.reshape(1, lat).astype(F32), ta, tb)


def _attn_kernel(q_ref, ka_ref, kb_ref, v_ref, o_ref, kcat, vt, qt, s_a, s_b, m_s, acc_s,
                 *, ones_rows):
    qi = pl.program_id(2)
    nch, ch, _ = kcat.shape
    bq = q_ref.shape[0]
    qw = q_ref.shape[1]

    @pl.when(qi == 0)
    def _():
        sum_rows = (lax.broadcasted_iota(jnp.int32, (BF16_ROWS, ch), 0) == 0).astype(BF16)

        def build(c, carry):
            r = pl.multiple_of(c * ch, ch)
            kcat[c, :, 0:LANES] = ka_ref[pl.ds(r, ch), :]
            kcat[c, :, LANES:2 * LANES] = kb_ref[pl.ds(r, ch), :]
            vt[c, 0:LANES, :] = v_ref[pl.ds(r, ch), :].astype(F32).T.astype(BF16)
            vt[c, LANES:LANES + BF16_ROWS, :] = sum_rows
            return carry
        lax.fori_loop(0, nch, build, 0)

    qt[0:qw, :] = q_ref[...].astype(F32).T.astype(BF16)
    if ones_rows:
        rowi = lax.broadcasted_iota(jnp.int32, (LANES, bq), 0)
        qt[LANES:2 * LANES, :] = jnp.where(rowi < ones_rows, 1.0, 0.0).astype(BF16)
    m_s[...] = jnp.full(m_s.shape, MASK_VALUE, F32)
    acc_s[...] = jnp.zeros(acc_s.shape, F32)

    def scores(c, dst):
        dst[...] = jnp.dot(kcat[c], qt[...], preferred_element_type=F32)

    def softmax_pv(src, c, key_offset):
        s = src[...]
        if key_offset is not None:
            key = lax.broadcasted_iota(jnp.int32, s.shape, 0) + key_offset
            qry = lax.broadcasted_iota(jnp.int32, s.shape, 1)
            s = jnp.where(key <= qry, s, MASK_VALUE)
        m_prev = m_s[...]
        m_new = jnp.maximum(m_prev, jnp.max(s, axis=0, keepdims=True))
        alpha = jnp.exp2(m_prev - m_new)
        p = jnp.exp2(s - m_new).astype(BF16)
        acc_s[...] = alpha * acc_s[...] + jnp.dot(vt[c], p, preferred_element_type=F32)
        m_s[...] = m_new

    assert bq == 2 * ch
    scores(0, s_a)

    def pair(p, carry):
        c = 2 * p
        scores(c + 1, s_b)
        softmax_pv(s_a, c, None)
        scores(c + 2, s_a)
        softmax_pv(s_b, c + 1, None)
        return carry
    lax.fori_loop(0, qi, pair, 0)

    scores(2 * qi + 1, s_b)
    softmax_pv(s_a, 2 * qi, 0)
    softmax_pv(s_b, 2 * qi + 1, ch)

    o_t = acc_s[0:LANES, :] / acc_s[LANES:LANES + 1, :]
    o_ref[...] = o_t.T.astype(o_ref.dtype)


def _attention(q_arr, q_cols, ka_arr, ka_col0, kb_arr, kb_per_head, v_arr, v_col0,
               batch, seq_len, heads, *, ones_rows, bq=1024):
    T = batch * seq_len
    bq = _pick(seq_len, (bq, 512, 256))
    ch = bq // 2
    nq = seq_len // bq
    nch = seq_len // ch
    blk = bq * q_cols * 2 + 3 * seq_len * LANES * 2 + bq * LANES * 2
    vrows = LANES + BF16_ROWS
    scratch_bytes = seq_len * 2 * LANES * 2 + seq_len * vrows * 2 + 2 * LANES * bq * 2 \
        + SUBLANES * bq * 4 + vrows * bq * 4 + 2 * ch * bq * 4
    temp = 6 * ch * bq * 4
    kb_map = (lambda b, h, i: (b * heads + h, 0)) if kb_per_head else (lambda b, h, i: (b, 0))
    return pl.pallas_call(
        functools.partial(_attn_kernel, ones_rows=ones_rows),
        grid=(batch, heads, nq),
        in_specs=[pl.BlockSpec((bq, q_cols), lambda b, h, i: (b * nq + i, h)),
                  pl.BlockSpec((seq_len, LANES), lambda b, h, i: (b, ka_col0 + h)),
                  pl.BlockSpec((seq_len, LANES), kb_map),
                  pl.BlockSpec((seq_len, LANES), lambda b, h, i: (b, v_col0 + h))],
        out_specs=pl.BlockSpec((bq, LANES), lambda b, h, i: (b * nq + i, h)),
        out_shape=jax.ShapeDtypeStruct((T, heads * LANES), BF16),
        scratch_shapes=[pltpu.VMEM((nch, ch, 2 * LANES), BF16),
                        pltpu.VMEM((nch, vrows, ch), BF16),
                        pltpu.VMEM((2 * LANES, bq), BF16),
                        pltpu.VMEM((ch, bq), F32),
                        pltpu.VMEM((ch, bq), F32),
                        pltpu.VMEM((1, bq), F32),
                        pltpu.VMEM((vrows, bq), F32)],
        compiler_params=pltpu.CompilerParams(
            dimension_semantics=("arbitrary", "arbitrary", "arbitrary"),
            vmem_limit_bytes=_vmem_limit(blk, scratch_bytes, temp)),
        name="attention",
    )(q_arr, ka_arr, kb_arr, v_arr)


def _split_bf16(x):
    pieces, rem = [], x
    for _ in range(FORGET_SPLITS - 1):
        t = rem * (2.0 ** 16 + 1.0)
        hi = t - (t - rem)
        pieces.append(hi.astype(BF16))
        rem = rem - hi
    pieces.append(rem.astype(BF16))
    return pieces


def _pad_cols(w, mult):
    pad = (-w.shape[-1]) % mult
    return w if pad == 0 else jnp.pad(w, [(0, 0)] * (w.ndim - 1) + [(0, pad)])


def _ffn(h, layer, seq_len, norm_pre_ffn, norm_post_ffn, ffn_w_gate, ffn_conv_w, ffn_conv_b,
         ffn_w_up, ffn_w_down):
    fpad = 1024
    f = ffn_w_gate.shape[2]
    fp = f + (-f) % fpad
    wg = _cast_pad(ffn_w_gate, layer, c_out=fp, name="cast_gate")
    wu = _cast_pad(ffn_w_up, layer, c_out=fp, name="cast_up")
    wd = _cast_pad(ffn_w_down, layer, r_out=fp, name="cast_down")
    cw = _pad_cols(ffn_conv_w[layer], fpad)
    cb = _pad_cols(ffn_conv_b[layer], fpad)
    a = _ffn_up(h, norm_pre_ffn[layer], wg, wu, cw, cb, seq_len)
    return _mm(a, wd, out_dtype=F32, post_gain=norm_post_ffn[layer], res=h, bk=512,
               name="ffn_down")


def kernel(x, positions, norm_pre_mix, norm_post_mix, norm_pre_ffn, norm_post_ffn, fox_w_qkvf, fox_b_f, fox_w_o, mla_w_dq, mla_q_norm, mla_w_uq, mla_w_o, kv_in_norm, mla_w_dkv, mla_kv_norm, mla_w_uk, mla_w_uv, ffn_w_gate, ffn_conv_w, ffn_conv_b, ffn_w_up, ffn_w_down):
    B, S, D = x.shape
    T = B * S
    H = D // LANES
    n_a = fox_w_qkvf.shape[0]
    n_b = mla_w_dq.shape[0]
    lat = mla_w_uk.shape[0]
    rope = mla_w_dkv.shape[1] - lat
    nope = mla_w_uk.shape[1] // H
    half = rope // 2
    assert nope == LANES and 2 * rope == LANES and mla_w_uv.shape[1] == D
    assert fox_w_qkvf.shape[2] == 3 * D + H and H <= LANES

    ffn_args = (norm_pre_ffn, norm_post_ffn, ffn_w_gate, ffn_conv_w, ffn_conv_b, ffn_w_up,
                ffn_w_down)
    h = x.reshape(T, D)

    for layer in range(n_a):
        fox_scale = LOG2E / math.sqrt(LANES)
        col_scale = jnp.concatenate([jnp.full((D,), fox_scale, F32), jnp.ones((2 * D,), F32)])
        wqkv = _cast_pad(fox_w_qkvf, layer, c_in=3 * D, scale=col_scale, name="cast_qkv")
        wf = _pad_cols(fox_w_qkvf[layer, :, 3 * D:], LANES).astype(BF16)
        bias = _pad_cols(fox_b_f[layer].reshape(1, H), LANES)
        gain = norm_pre_mix[layer]
        qkv = _mm(h, wqkv, out_dtype=BF16, pre_gain=gain, name="fox_qkv")
        flog = _mm(h, wf, out_dtype=F32, pre_gain=gain, name="fox_gate")
        c = _cumgate(flog, bias, B, S)
        neg = (-LOG2E) * c[:, :H].reshape(B, S, H).transpose(0, 2, 1)
        kb = _pad_cols(jnp.stack(_split_bf16(neg), axis=-1), LANES).reshape(B * H * S, LANES)
        o = _attention(qkv, LANES, qkv, H, kb, True, qkv, 2 * H, B, S, H,
                       ones_rows=FORGET_SPLITS)
        h = _mm(o, _cast_pad(fox_w_o, layer, name="cast_wo"), out_dtype=F32,
                post_gain=norm_post_mix[layer], res=h, bk=512, name="fox_out")
        h = _ffn(h, layer, S, *ffn_args)

    if n_b:
        inv_freq = ROPE_THETA ** (-jnp.arange(0, rope, 2, dtype=F32) / rope)
        ang = positions.astype(F32).reshape(T, 1) * inv_freq[None, :]
        ang4 = jnp.tile(ang, (1, 4))
        one, zero = jnp.ones((half,), F32), jnp.zeros((half,), F32)
        sgn = jnp.concatenate([-one, one, -one, one]).reshape(1, LANES)
        selc = jnp.concatenate([one, one, zero, zero]).reshape(1, LANES)
        sels = jnp.concatenate([zero, zero, -one, one]).reshape(1, LANES)
        ta, tb, tq_rope = _rope_tables(ang4, sgn, selc, sels)
        mla_scale = LOG2E / math.sqrt(nope + rope)
        tq = mla_scale * jnp.concatenate([jnp.ones((T, nope), F32), tq_rope], axis=1)

        y1, y2 = mla_w_dkv[:, lat:lat + half], mla_w_dkv[:, lat + half:]
        wdkv = jnp.concatenate([mla_w_dkv[:, :lat], y1, y2, y1, y2, y2, y1, y2, y1],
                               axis=1).astype(BF16)
        ckv, kr = _dkv(h, kv_in_norm, wdkv, mla_kv_norm, ta, tb, lat)
        wkv = jnp.concatenate([mla_w_uk, mla_w_uv], axis=1).astype(BF16)
        kv = _mm(ckv, wkv, out_dtype=BF16, name="mla_kv_up")

        for jb in range(n_b):
            layer = n_a + jb
            cq = _mm(h, _cast_pad(mla_w_dq, jb, name="cast_dq"), out_dtype=BF16,
                     pre_gain=norm_pre_mix[layer], post_gain=mla_q_norm[jb], name="mla_dq")
            wq = mla_w_uq[jb].reshape(-1, H, nope + rope)
            x1, x2 = wq[..., nope:nope + half], wq[..., nope + half:]
            wuq = jnp.concatenate([wq[..., :nope], x1, x2, x2, x1], axis=-1)
            wuq = wuq.reshape(-1, H * 2 * LANES).astype(BF16)
            q = _mm(cq, wuq, out_dtype=BF16, mul=tq, name="mla_uq")
            o = _attention(q, 2 * LANES, kv, 0, kr, False, kv, H, B, S, H, ones_rows=0)
            h = _mm(o, _cast_pad(mla_w_o, jb, name="cast_wo"), out_dtype=F32,
                    post_gain=norm_post_mix[layer], res=h, bk=512, name="mla_out")
            h = _ffn(h, layer, S, *ffn_args)

    return h.reshape(B, S, D)
```
